```python
import jax, jax.numpy as jnp
from jax import lax
import numpy as np

D_MODEL = 4096
BATCH = 2
SEQ = 4096
DEPTH = 1
DEC_BATCH = 32
DEC_SEQ = 4
PAST_LEN = 8192
PAGE_SIZE = 128

HEAD_DIM = 128
A_HEADS = 16
B_HEADS = 16
B_GROUPS = 4
A_WIDTH = A_HEADS * HEAD_DIM
B_WIDTH = B_HEADS * HEAD_DIM
B_KV_WIDTH = B_GROUPS * HEAD_DIM
MOBA_BLOCK = 256
MOBA_TOPK = 3
CMP_LEN = 32
CMP_STRIDE = 16
CMP_HIDDEN = 256
SLC_BLOCK = 64
SLC_TOPN = 16
WINDOW = 512
WIN_QBLOCK = 128
Q_CHUNK = 32
D_FF = ((8 * D_MODEL + 3 * 256 - 1) // (3 * 256)) * 256
ROPE_THETA = 10000.0
LN_EPS = 1e-5
ALPHA = (2 * DEPTH) ** 0.25
BETA = (8 * DEPTH) ** -0.25
NEG = -1e30
FORCE = 1e9
SCALE = HEAD_DIM ** -0.5
IN_SIZES = (A_WIDTH, A_WIDTH, A_WIDTH, B_WIDTH, B_KV_WIDTH, B_KV_WIDTH, B_KV_WIDTH, B_KV_WIDTH, B_KV_WIDTH, B_KV_WIDTH, 3 * B_HEADS, 2 * D_MODEL)
IN_COLS = sum(IN_SIZES)

kernel_name = 'moba_nsa_griffin_merge_deepnorm_step'


def layer_norm(x, g, b):
    xf = x.astype(jnp.float32)
    mu = jnp.mean(xf, axis=-1, keepdims=True)
    var = jnp.mean(jnp.square(xf - mu), axis=-1, keepdims=True)
    return ((xf - mu) * lax.rsqrt(var + LN_EPS) * g + b).astype(x.dtype)


def rope(x, pos):
    inv = ROPE_THETA ** (-jnp.arange(0, HEAD_DIM, 2, dtype=jnp.float32) / HEAD_DIM)
    ang = pos.astype(jnp.float32)[:, None] * inv[None, :]
    cos, sin = jnp.cos(ang)[:, None, :], jnp.sin(ang)[:, None, :]
    x1, x2 = jnp.split(x.astype(jnp.float32), 2, axis=-1)
    return jnp.concatenate([x1 * cos - x2 * sin, x2 * cos + x1 * sin], axis=-1).astype(x.dtype)


def masked_softmax(s, mask):
    s = jnp.where(mask, s, NEG)
    m = jnp.max(s, axis=-1, keepdims=True)
    p = jnp.where(mask, jnp.exp(s - m), 0.0)
    l = jnp.sum(p, axis=-1, keepdims=True)
    safe = jnp.where(l > 0, l, 1.0)
    lse = jnp.where(l > 0, m + jnp.log(safe), NEG)[..., 0]
    return p / safe, lse


def merge(o1, l1, o2, l2):
    m = jnp.maximum(l1, l2)
    w1 = jnp.exp(l1 - m)[..., None]
    w2 = jnp.exp(l2 - m)[..., None]
    return ((o1.astype(jnp.float32) * w1 + o2.astype(jnp.float32) * w2) / (w1 + w2)).astype(o1.dtype)


def dense_masked_attention(q, k, v, mask):
    B, S, H, D = q.shape
    G = k.shape[2]
    qg = q.reshape(B, S, G, H // G, D)
    s = jnp.einsum('bsgrd,blgd->bgrsl', qg, k, preferred_element_type=jnp.float32) * SCALE
    p, lse = masked_softmax(s, mask)
    o = jnp.einsum('bgrsl,blgd->bsgrd', p.astype(v.dtype), v).reshape(B, S, H, D)
    return o, lse.transpose(0, 3, 1, 2).reshape(B, S, H), p


def local_block_attention(q, kv, blk):
    B, T, H, D = q.shape
    G = kv.shape[3]
    nb = -(-T // blk)
    pad = nb * blk - T
    if pad:
        q = jnp.pad(q, ((0, 0), (0, pad), (0, 0), (0, 0)))
        kv = jnp.pad(kv, ((0, 0), (0, pad), (0, 0), (0, 0), (0, 0)))
    qb = q.reshape(B, nb, blk, G, H // G, D)
    kb = kv.reshape(B, nb, blk, 2, G, D)
    s = jnp.einsum('bnqgrd,bnkgd->bngrqk', qb, kb[:, :, :, 0], preferred_element_type=jnp.float32) * SCALE
    p, lse = masked_softmax(s, jnp.tril(jnp.ones((blk, blk), dtype=bool)))
    o = jnp.einsum('bngrqk,bnkgd->bnqgrd', p.astype(kv.dtype), kb[:, :, :, 1])
    o = o.reshape(B, nb * blk, H, D)[:, :T]
    lse = lse.transpose(0, 1, 4, 2, 3).reshape(B, nb * blk, H)[:, :T]
    return o, lse


def gathered_attention(q, k_sel, v_sel, valid):
    Q, G, K, R, D = k_sel.shape
    r = q.shape[1] // G
    qg = q.reshape(Q, G, r, D)
    s = jnp.einsum('qgrd,qgknd->qgrkn', qg, k_sel, preferred_element_type=jnp.float32) * SCALE
    p, lse = masked_softmax(s.reshape(Q, G, r, K * R), jnp.repeat(valid, R, axis=-1)[:, :, None, :])
    o = jnp.einsum('qgrkn,qgknd->qgrd', p.reshape(Q, G, r, K, R).astype(v_sel.dtype), v_sel)
    return o.reshape(Q, G * r, D), lse.reshape(Q, G * r)


def gathered_prompt(q, kv, idx, valid, blk):
    B, T, H, D = q.shape
    G, K = idx.shape[2], idx.shape[3]
    qc = Q_CHUNK if T % Q_CHUNK == 0 else T
    nc = T // qc
    g_ar = jnp.arange(G)[None, :, None, None]

    def body(args):
        b, qq, ii, vv = args
        pos = ii[..., None] * blk + jnp.arange(blk)
        rows = kv[b, pos, :, g_ar]
        return gathered_attention(qq, rows[..., 0, :], rows[..., 1, :], vv)

    xs = (jnp.repeat(jnp.arange(B), nc), q.reshape(B * nc, qc, H, D),
          idx.reshape(B * nc, qc, G, K), valid.reshape(B * nc, qc, G, K))
    o, lse = lax.map(body, xs)
    return o.reshape(B, T, H, D), lse.reshape(B, T, H)


def gathered_paged(q, page_table, idx, valid, blk, fetch):
    def body(args):
        pt, qq, ii, vv = args
        pos = ii[..., None] * blk + jnp.arange(blk)
        k_sel, v_sel = fetch(pt[pos // PAGE_SIZE], pos % PAGE_SIZE)
        return gathered_attention(qq, k_sel, v_sel, vv)

    return lax.map(body, (page_table, q, idx, valid))


def own_block_sample(q, k_new, v_new, blk, fetch_rows, page_table):
    S = q.shape[1]
    r_c = PAST_LEN % blk
    if r_c > 0:
        cpos = (PAST_LEN - r_c) + jnp.arange(r_c)
        k_old, v_old = fetch_rows(page_table[:, cpos // PAGE_SIZE], (cpos % PAGE_SIZE)[None, :])
        k_new = jnp.concatenate([k_old, k_new], axis=1)
        v_new = jnp.concatenate([v_old, v_new], axis=1)
    mask = jnp.arange(r_c + S)[None, :] <= r_c + jnp.arange(S)[:, None]
    o, lse, _ = dense_masked_attention(q, k_new, v_new, mask)
    return o, lse


def moba_select(q, kmean, q_pos):
    n_sel = kmean.shape[1]
    s = jnp.einsum('bshd,bjhd->bshj', q.astype(jnp.float32), kmean)
    cand = jnp.arange(n_sel)[None, :] < (q_pos // MOBA_BLOCK)[:, None]
    s = jnp.where(cand[None, :, None, :], s, NEG)
    vals, idx = lax.top_k(s, min(MOBA_TOPK, n_sel))
    return idx, vals > NEG / 2


def moba_prompt(q, k, v):
    B, T, H, D = q.shape
    kv = jnp.stack([k, v], axis=2)
    o_loc, lse_loc = local_block_attention(q, kv, MOBA_BLOCK)
    n_sel = (T - 1) // MOBA_BLOCK
    if n_sel == 0:
        return o_loc
    kmean = jnp.mean(k[:, :n_sel * MOBA_BLOCK].reshape(B, n_sel, MOBA_BLOCK, H, D), axis=2, dtype=jnp.float32)
    idx, valid = moba_select(q, kmean, jnp.arange(T))
    o_sel, lse_sel = gathered_prompt(q, kv, idx, valid, MOBA_BLOCK)
    return merge(o_loc, lse_loc, o_sel, lse_sel)


def moba_sample(q, k, v, cache_moba_k, cache_moba_v, page_table):
    DB, S, H, D = q.shape
    fetch_rows = lambda pages, off: (cache_moba_k[pages, off], cache_moba_v[pages, off])
    o_loc, lse_loc = own_block_sample(q, k, v, MOBA_BLOCK, fetch_rows, page_table)
    n_sel = PAST_LEN // MOBA_BLOCK
    if n_sel == 0:
        return o_loc
    ppb = MOBA_BLOCK // PAGE_SIZE

    def block_means(pt):
        rows = cache_moba_k[pt[:n_sel * ppb]]
        return jnp.mean(rows.reshape(n_sel, MOBA_BLOCK, H, D), axis=1, dtype=jnp.float32)

    kmean = lax.map(block_means, page_table)
    idx, valid = moba_select(q, kmean, PAST_LEN + jnp.arange(S))
    h_ar = jnp.arange(H)[None, :, None, None]
    fetch_sel = lambda pages, off: (cache_moba_k[pages, off, h_ar], cache_moba_v[pages, off, h_ar])
    o_sel, lse_sel = gathered_paged(q, page_table, idx, valid, MOBA_BLOCK, fetch_sel)
    return merge(o_loc, lse_loc, o_sel, lse_sel)


def compress(rows, cmp_pos, cmp_w1, cmp_w2):
    B, L = rows.shape[:2]
    G, D = rows.shape[3], rows.shape[4]
    R = CMP_LEN // CMP_STRIDE
    nch = L // CMP_STRIDE
    n_cmp = nch - R + 1
    ch = rows[:, :nch * CMP_STRIDE].reshape(B, nch, CMP_STRIDE, 2, G, D)
    w1r = cmp_w1.reshape(2, R, CMP_STRIDE, D, CMP_HIDDEN)
    part = jnp.einsum('bcsegd,ersdh->bcregh', ch, w1r)
    pos_bias = jnp.einsum('eld,eldh->eh', cmp_pos, cmp_w1.reshape(2, CMP_LEN, D, CMP_HIDDEN))
    pre = pos_bias[:, None, :]
    for r in range(R):
        pre = pre + part[:, r:r + n_cmp, r]
    out = jnp.einsum('biegh,ehd->biegd', jax.nn.gelu(pre), cmp_w2)
    return out[:, :, 0], out[:, :, 1]


def compressed_attention(q, k_cmp, v_cmp, q_pos):
    n_cmp = k_cmp.shape[1]
    ends = jnp.arange(n_cmp) * CMP_STRIDE + (CMP_LEN - 1)
    o, _, p = dense_masked_attention(q, k_cmp, v_cmp, ends[None, :] <= q_pos[:, None])
    return o, jnp.sum(p, axis=2)


def nsa_select(imp, q_pos, n_sel):
    n_cmp = imp.shape[-1]
    c_start = jnp.arange(n_cmp) * CMP_STRIDE
    s_start = jnp.arange(n_sel) * SLC_BLOCK
    overlap = ((c_start[:, None] < s_start[None, :] + SLC_BLOCK) & (c_start[:, None] + CMP_LEN > s_start[None, :])).astype(jnp.float32)
    score = jnp.einsum('bgsi,ij->bsgj', imp, overlap)
    own = (q_pos // SLC_BLOCK)[:, None]
    j = jnp.arange(n_sel)[None, :]
    forced = (j == 0) | (j == own - 1)
    score = jnp.where((j < own)[None, :, None, :], jnp.where(forced[None, :, None, :], FORCE, score), NEG)
    vals, idx = lax.top_k(score, min(SLC_TOPN - 1, n_sel))
    return idx, vals > NEG / 2


def window_prompt(q, kv):
    B, T, H, D = q.shape
    G = kv.shape[3]
    nb = T // WIN_QBLOCK
    span = WINDOW + WIN_QBLOCK
    kv_pad = jnp.pad(kv, ((0, 0), (WINDOW, 0), (0, 0), (0, 0), (0, 0)))
    idx = jnp.arange(nb)[:, None] * WIN_QBLOCK + jnp.arange(span)[None, :]
    kb = kv_pad[:, idx]
    qb = q.reshape(B, nb, WIN_QBLOCK, G, H // G, D)
    q_pos = jnp.arange(nb)[:, None, None] * WIN_QBLOCK + jnp.arange(WIN_QBLOCK)[None, :, None]
    k_pos = jnp.arange(nb)[:, None, None] * WIN_QBLOCK - WINDOW + jnp.arange(span)[None, None, :]
    mask = (k_pos >= 0) & (k_pos <= q_pos) & (q_pos - k_pos < WINDOW)
    s = jnp.einsum('bnqgrd,bnkgd->bngrqk', qb, kb[:, :, :, 0], preferred_element_type=jnp.float32) * SCALE
    p, _ = masked_softmax(s, mask[None, :, None, None])
    o = jnp.einsum('bngrqk,bnkgd->bnqgrd', p.astype(kv.dtype), kb[:, :, :, 1])
    return o.reshape(B, T, H, D)


def window_sample(q, kv_new, buf):
    S = q.shape[1]
    WB = buf.shape[1]
    kv = jnp.concatenate([buf, kv_new], axis=1)
    k_pos = PAST_LEN - WB + jnp.arange(WB + S)
    q_pos = PAST_LEN + jnp.arange(S)
    mask = (k_pos[None, :] <= q_pos[:, None]) & (q_pos[:, None] - k_pos[None, :] < WINDOW)
    o, _, _ = dense_masked_attention(q, kv[:, :, 0], kv[:, :, 1], mask)
    return o, kv[:, S:]


def gate_branches(g_nsa, o_c, o_s, o_w):
    B, T, H, _ = o_c.shape
    g = jax.nn.sigmoid(g_nsa).reshape(B, T, 3, H, 1)
    return g[:, :, 0] * o_c + g[:, :, 1] * o_s + g[:, :, 2] * o_w


def nsa_prompt(q, kv_cmp, kv_slc, kv_win, g_nsa, cmp_pos, cmp_w1, cmp_w2):
    T = q.shape[1]
    pos = jnp.arange(T)
    k_cmp, v_cmp = compress(kv_cmp, cmp_pos, cmp_w1, cmp_w2)
    o_c, imp = compressed_attention(q, k_cmp, v_cmp, pos)
    o_s, lse_loc = local_block_attention(q, kv_slc, SLC_BLOCK)
    n_sel = (T - 1) // SLC_BLOCK
    if n_sel > 0:
        idx, valid = nsa_select(imp, pos, n_sel)
        o_sel, lse_sel = gathered_prompt(q, kv_slc, idx, valid, SLC_BLOCK)
        o_s = merge(o_s, lse_loc, o_sel, lse_sel)
    o_w = window_prompt(q, kv_win)
    return gate_branches(g_nsa, o_c, o_s, o_w)


def nsa_sample(q, kv_cmp, kv_slc, kv_win, g_nsa, cache_nsa_kv, state_win_kv, page_table, cmp_pos, cmp_w1, cmp_w2):
    DB, S, H, D = q.shape
    G = kv_cmp.shape[3]
    q_pos = PAST_LEN + jnp.arange(S)

    def cmp_tokens(args):
        pt, new = args
        past = cache_nsa_kv[pt, :, 0:2].reshape(PAST_LEN, 2, G, D)
        k_c, v_c = compress(jnp.concatenate([past, new], axis=0)[None], cmp_pos, cmp_w1, cmp_w2)
        return k_c[0], v_c[0]

    k_cmp, v_cmp = lax.map(cmp_tokens, (page_table, kv_cmp))
    o_c, imp = compressed_attention(q, k_cmp, v_cmp, q_pos)

    def fetch_rows(pages, off):
        rows = cache_nsa_kv[pages, off]
        return rows[:, :, 2], rows[:, :, 3]

    o_s, lse_loc = own_block_sample(q, kv_slc[:, :, 0], kv_slc[:, :, 1], SLC_BLOCK, fetch_rows, page_table)
    n_sel = PAST_LEN // SLC_BLOCK
    g_ar = jnp.arange(G)[None, :, None, None]

    def fetch_sel(pages, off):
        rows = cache_nsa_kv[pages, off, 2:4, g_ar]
        return rows[..., 0, :], rows[..., 1, :]

    idx, valid = nsa_select(imp, q_pos, n_sel)
    o_sel, lse_sel = gathered_paged(q, page_table, idx, valid, SLC_BLOCK, fetch_sel)
    o_s = merge(o_s, lse_loc, o_sel, lse_sel)
    o_w, new_buf = window_sample(q, kv_win, state_win_kv)
    return gate_branches(g_nsa, o_c, o_s, o_w), new_buf


def project(x, pos, w_in):
    B, T, _ = x.shape
    proj = jnp.einsum('btm,mn->btn', x, w_in)
    offs = np.cumsum(IN_SIZES)[:-1].tolist()
    q_a, k_a, v_a, q_b, kc, vc, ks, vs, kw, vw, g_nsa, g_merge = jnp.split(proj, offs, axis=-1)
    heads = lambda t, n: t.reshape(B, T, n, HEAD_DIM)
    q_a = rope(heads(q_a, A_HEADS), pos)
    k_a = rope(heads(k_a, A_HEADS), pos)
    v_a = heads(v_a, A_HEADS)
    q_b = rope(heads(q_b, B_HEADS), pos)
    kv_cmp = jnp.stack([rope(heads(kc, B_GROUPS), pos), heads(vc, B_GROUPS)], axis=2)
    kv_slc = jnp.stack([rope(heads(ks, B_GROUPS), pos), heads(vs, B_GROUPS)], axis=2)
    kv_win = jnp.stack([rope(heads(kw, B_GROUPS), pos), heads(vw, B_GROUPS)], axis=2)
    return q_a, k_a, v_a, q_b, kv_cmp, kv_slc, kv_win, g_nsa, g_merge


def finish(x, y_a, y_b, g_merge, w_up_a, w_up_b, w_o, ln1_g, ln1_b, w_gate, w_up, w_down, ln2_g, ln2_b):
    B, T = x.shape[:2]
    g_a, g_b = jnp.split(jax.nn.sigmoid(g_merge), 2, axis=-1)
    m = g_a * (y_a.reshape(B, T, A_WIDTH) @ w_up_a) + g_b * (y_b.reshape(B, T, B_WIDTH) @ w_up_b)
    h = layer_norm(ALPHA * x + m @ w_o, ln1_g, ln1_b)
    f = (jax.nn.silu(h @ w_gate) * (h @ w_up)) @ w_down
    return layer_norm(ALPHA * h + f, ln2_g, ln2_b)


def setup_inputs(seed: int = 0) -> dict:
    key = jax.random.key(seed)
    ks = jax.random.split(key, 24)
    n_pages = PAST_LEN // PAGE_SIZE
    n_pool = (DEC_BATCH * n_pages * 5 + 3) // 4
    wb = min(WINDOW, PAST_LEN)
    nrm = lambda k, shape, scale=1.0: jax.random.normal(k, shape, jnp.float32) * scale
    page_table = jax.random.permutation(ks[0], n_pool)[:DEC_BATCH * n_pages].reshape(DEC_BATCH, n_pages).astype(jnp.int32)
    return {
        'x_prompt': nrm(ks[1], (BATCH, SEQ, D_MODEL)),
        'x_sample': nrm(ks[2], (DEC_BATCH, DEC_SEQ, D_MODEL)),
        'cache_moba_k': nrm(ks[3], (n_pool, PAGE_SIZE, A_HEADS, HEAD_DIM)),
        'cache_moba_v': nrm(ks[4], (n_pool, PAGE_SIZE, A_HEADS, HEAD_DIM)),
        'cache_nsa_kv': nrm(ks[5], (n_pool, PAGE_SIZE, 4, B_GROUPS, HEAD_DIM)),
        'state_win_kv': nrm(ks[6], (DEC_BATCH, wb, 2, B_GROUPS, HEAD_DIM)),
        'page_table': page_table,
        'w_in': nrm(ks[7], (D_MODEL, IN_COLS), D_MODEL ** -0.5),
        'cmp_pos': nrm(ks[8], (2, CMP_LEN, HEAD_DIM), 0.2),
        'cmp_w1': nrm(ks[9], (2, CMP_LEN * HEAD_DIM, CMP_HIDDEN), (CMP_LEN * HEAD_DIM) ** -0.5),
        'cmp_w2': nrm(ks[10], (2, CMP_HIDDEN, HEAD_DIM), CMP_HIDDEN ** -0.5),
        'w_up_a': nrm(ks[11], (A_WIDTH, D_MODEL), A_WIDTH ** -0.5),
        'w_up_b': nrm(ks[12], (B_WIDTH, D_MODEL), B_WIDTH ** -0.5),
        'w_o': nrm(ks[13], (D_MODEL, D_MODEL), BETA * D_MODEL ** -0.5),
        'ln1_g': 1.0 + nrm(ks[14], (D_MODEL,), 0.05),
        'ln1_b': nrm(ks[15], (D_MODEL,), 0.05),
        'w_gate': nrm(ks[16], (D_MODEL, D_FF), D_MODEL ** -0.5),
        'w_up': nrm(ks[17], (D_MODEL, D_FF), D_MODEL ** -0.5),
        'w_down': nrm(ks[18], (D_FF, D_MODEL), BETA * D_FF ** -0.5),
        'ln2_g': 1.0 + nrm(ks[19], (D_MODEL,), 0.05),
        'ln2_b': nrm(ks[20], (D_MODEL,), 0.05),
    }


def reference(x_prompt, x_sample, cache_moba_k, cache_moba_v, cache_nsa_kv, state_win_kv, page_table,
              w_in, cmp_pos, cmp_w1, cmp_w2, w_up_a, w_up_b, w_o, ln1_g, ln1_b,
              w_gate, w_up, w_down, ln2_g, ln2_b):
    pos_p = jnp.arange(x_prompt.shape[1])
    pos_s = PAST_LEN + jnp.arange(x_sample.shape[1])
    y_prompt, y_sample = x_prompt, x_sample
    for _ in range(DEPTH):
        qa, ka, va, qb, kvc, kvs, kvw, gn, gm = project(y_prompt, pos_p, w_in)
        ya = moba_prompt(qa, ka, va)
        yb = nsa_prompt(qb, kvc, kvs, kvw, gn, cmp_pos, cmp_w1, cmp_w2)
        moba_k_prompt, moba_v_prompt = ka, va
        nsa_kv_prompt = jnp.concatenate([kvc, kvs], axis=2)
        win_kv_prompt = kvw[:, kvw.shape[1] - min(WINDOW, kvw.shape[1]):]
        y_prompt = finish(y_prompt, ya, yb, gm, w_up_a, w_up_b, w_o, ln1_g, ln1_b, w_gate, w_up, w_down, ln2_g, ln2_b)
        qa, ka, va, qb, kvc, kvs, kvw, gn, gm = project(y_sample, pos_s, w_in)
        ya = moba_sample(qa, ka, va, cache_moba_k, cache_moba_v, page_table)
        yb, win_kv_sample = nsa_sample(qb, kvc, kvs, kvw, gn, cache_nsa_kv, state_win_kv, page_table, cmp_pos, cmp_w1, cmp_w2)
        moba_k_sample, moba_v_sample = ka, va
        nsa_kv_sample = jnp.concatenate([kvc, kvs], axis=2)
        y_sample = finish(y_sample, ya, yb, gm, w_up_a, w_up_b, w_o, ln1_g, ln1_b, w_gate, w_up, w_down, ln2_g, ln2_b)
    return (y_prompt, y_sample, moba_k_prompt, moba_v_prompt, nsa_kv_prompt, win_kv_prompt,
            moba_k_sample, moba_v_sample, nsa_kv_sample, win_kv_sample)
```

```python
import functools

import jax
import jax.numpy as jnp
import numpy as np
from jax import lax
from jax.experimental import pallas as pl
from jax.experimental.pallas import tpu as pltpu

F32 = jnp.float32
BF16 = jnp.bfloat16

HEAD_DIM = 128
A_HEADS = 16
B_HEADS = 16
B_GROUPS = 4
GROUP_HEADS = B_HEADS // B_GROUPS
MOBA_BLOCK = 256
MOBA_TOPK = 3
CMP_LEN = 32
CMP_STRIDE = 16
CMP_HIDDEN = 256
SLC_BLOCK = 64
SLC_TOPN = 16
WINDOW = 512
PAGE_SIZE = 128
ROPE_THETA = 10000.0
LN_EPS = 1e-5
DEPTH = 1
ALPHA = (2 * DEPTH) ** 0.25
NEG = -1e30
FORCE = 1e9
SCALE = HEAD_DIM ** -0.5

LANES = 128
VMEM_LIMIT = 52 * 1024 * 1024

COL_QA, COL_KA, COL_VA, COL_QB = 0, 16, 32, 48
COL_KC, COL_VC, COL_KS, COL_VS, COL_KW, COL_VW = 64, 68, 72, 76, 80, 84
MAIN_COLS = 88 * HEAD_DIM
GATE_NSA_COLS = 3 * B_HEADS


def _params(sem, vmem=VMEM_LIMIT):
    return pltpu.CompilerParams(dimension_semantics=sem, vmem_limit_bytes=vmem)


def _dot(a, b):
    return jnp.dot(a, b, preferred_element_type=F32)


def _dot_nt(a, b):
    return lax.dot_general(a, b, (((1,), (1,)), ((), ())), preferred_element_type=F32)


def _dot_tn(a, b):
    return lax.dot_general(a, b, (((0,), (0,)), ((), ())), preferred_element_type=F32)


def _split3(a):
    a1 = a.astype(BF16)
    r1 = a - a1.astype(F32)
    a2 = r1.astype(BF16)
    a3 = (r1 - a2.astype(F32)).astype(BF16)
    return a1, a2, a3


def _dot_hi_lhs(a, b_bf16, dot=_dot):
    a1, a2, a3 = _split3(a)
    return dot(a1, b_bf16) + dot(a2, b_bf16) + dot(a3, b_bf16)


def _dot_hi_rhs(a_bf16, b, dot=_dot):
    b1, b2, b3 = _split3(b)
    return dot(a_bf16, b1) + dot(a_bf16, b2) + dot(a_bf16, b3)


def _dot_hi(a, b, dot=_dot):
    a1, a2, a3 = _split3(a)
    b1, b2, b3 = _split3(b)
    return (dot(a1, b1) + dot(a1, b2) + dot(a2, b1)) + (dot(a1, b3) + dot(a2, b2) + dot(a3, b1))


def _rank_select(s, idx, n, topk, axis):
    rank = jnp.zeros(s.shape, jnp.int32)
    for k in range(n):
        sk = s[k:k + 1, :] if axis == 0 else s[:, k:k + 1]
        rank = rank + jnp.where(idx > k, (sk >= s).astype(jnp.int32), (sk > s).astype(jnp.int32))
    return jnp.where((rank < topk) & (s > NEG / 2), 1.0, 0.0).astype(F32)


def _layer_norm(v, g, b):
    mu = jnp.mean(v, axis=-1, keepdims=True)
    d = v - mu
    var = jnp.mean(d * d, axis=-1, keepdims=True)
    return d * lax.rsqrt(var + LN_EPS) * g + b


def _proj_kernel(x_ref, w_ref, cos_ref, sin_ref, o_ref, *, rope_tiles, tn):
    acc = _dot(x_ref[...], w_ref[...])
    if not rope_tiles:
        o_ref[...] = acc
        return
    j = pl.program_id(1)
    is_rope = functools.reduce(jnp.logical_or, [(j >= lo) & (j < hi) for lo, hi in rope_tiles])

    @pl.when(is_rope)
    def _():
        cos = cos_ref[...]
        sin = sin_ref[...]
        for h in range(tn // HEAD_DIM):
            a = acc[:, h * HEAD_DIM:(h + 1) * HEAD_DIM]
            o_ref[:, h * HEAD_DIM:(h + 1) * HEAD_DIM] = a * cos + pltpu.roll(a, HEAD_DIM // 2, 1) * sin

    @pl.when(jnp.logical_not(is_rope))
    def _():
        o_ref[...] = acc


def _project(xb, w, cos, sin, rope_tiles, tm, tn):
    m, k = xb.shape
    n = w.shape[1]
    return pl.pallas_call(
        functools.partial(_proj_kernel, rope_tiles=rope_tiles, tn=tn),
        grid=(m // tm, n // tn),
        in_specs=[pl.BlockSpec((tm, k), lambda i, j: (i, 0)),
                  pl.BlockSpec((k, tn), lambda i, j: (0, j)),
                  pl.BlockSpec((tm, HEAD_DIM), lambda i, j: (i, 0)),
                  pl.BlockSpec((tm, HEAD_DIM), lambda i, j: (i, 0))],
        out_specs=pl.BlockSpec((tm, tn), lambda i, j: (i, j)),
        out_shape=jax.ShapeDtypeStruct((m, n), F32),
        compiler_params=_params(("parallel", "arbitrary")),
        name="project",
    )(xb, w, cos, sin)


def _moba_prompt_kernel(q_ref, k_ref, v_ref, o_ref, kb_scr, vb_scr, kmean_scr, m_scr, l_scr, acc_scr, *, nblk):
    i = pl.program_id(2)
    blk = MOBA_BLOCK

    @pl.when(i == 0)
    def _():
        for j in range(nblk):
            kk = k_ref[j * blk:(j + 1) * blk, :]
            kmean_scr[j:j + 1, :] = jnp.mean(kk, axis=0, keepdims=True)
            kb_scr[j * blk:(j + 1) * blk, :] = kk.astype(BF16)
            vb_scr[j * blk:(j + 1) * blk, :] = v_ref[j * blk:(j + 1) * blk, :].astype(BF16)

    q = q_ref[...]
    qb = q.astype(BF16)
    gate = _dot_hi(q, kmean_scr[...], _dot_nt)
    jdx = lax.broadcasted_iota(jnp.int32, gate.shape, 1)
    gate = jnp.where(jdx < i, gate, NEG)
    sel = _rank_select(gate, jdx, nblk, MOBA_TOPK, axis=1)

    start = pl.multiple_of(i * blk, blk)
    s = _dot_nt(qb, kb_scr[pl.ds(start, blk), :]) * SCALE
    r = lax.broadcasted_iota(jnp.int32, s.shape, 0)
    c = lax.broadcasted_iota(jnp.int32, s.shape, 1)
    s = jnp.where(c <= r, s, NEG)
    m = jnp.max(s, axis=1, keepdims=True)
    p = jnp.exp(s - m)
    m_scr[...] = m
    l_scr[...] = jnp.sum(p, axis=1, keepdims=True)
    acc_scr[...] = _dot(p.astype(BF16), vb_scr[pl.ds(start, blk), :])

    def body(j, carry):
        st = pl.multiple_of(j * blk, blk)
        sj = _dot_nt(qb, kb_scr[pl.ds(st, blk), :]) * SCALE
        chosen = jnp.sum(jnp.where(jdx == j, sel, 0.0), axis=1, keepdims=True)
        sj = jnp.where(chosen > 0.5, sj, NEG)
        m_old = m_scr[...]
        m_new = jnp.maximum(m_old, jnp.max(sj, axis=1, keepdims=True))
        alpha = jnp.exp(m_old - m_new)
        pj = jnp.exp(sj - m_new)
        l_scr[...] = alpha * l_scr[...] + jnp.sum(pj, axis=1, keepdims=True)
        acc_scr[...] = alpha * acc_scr[...] + _dot(pj.astype(BF16), vb_scr[pl.ds(st, blk), :])
        m_scr[...] = m_new
        return carry

    lax.fori_loop(0, i, body, 0)
    o_ref[...] = (acc_scr[...] / l_scr[...]).astype(o_ref.dtype)


def _moba_prompt(p_all, batch, seq):
    nblk = seq // MOBA_BLOCK
    blk = MOBA_BLOCK
    return pl.pallas_call(
        functools.partial(_moba_prompt_kernel, nblk=nblk),
        grid=(batch, A_HEADS, nblk),
        in_specs=[pl.BlockSpec((blk, HEAD_DIM), lambda b, h, i: (b * nblk + i, COL_QA + h)),
                  pl.BlockSpec((seq, HEAD_DIM), lambda b, h, i: (b, COL_KA + h)),
                  pl.BlockSpec((seq, HEAD_DIM), lambda b, h, i: (b, COL_VA + h))],
        out_specs=pl.BlockSpec((blk, HEAD_DIM), lambda b, h, i: (b * nblk + i, h)),
        out_shape=jax.ShapeDtypeStruct((batch * seq, A_HEADS * HEAD_DIM), BF16),
        scratch_shapes=[pltpu.VMEM((seq, HEAD_DIM), BF16), pltpu.VMEM((seq, HEAD_DIM), BF16),
                        pltpu.VMEM((nblk, HEAD_DIM), F32), pltpu.VMEM((blk, 1), F32),
                        pltpu.VMEM((blk, 1), F32), pltpu.VMEM((blk, HEAD_DIM), F32)],
        compiler_params=_params(("parallel", "parallel", "arbitrary")),
        name="moba_prompt",
    )(p_all, p_all, p_all)


CMP_PAGES = 16
CMP_CHUNKS = CMP_PAGES * PAGE_SIZE // CMP_STRIDE


def _compress_kernel(pt_ref, *refs, row_cols):
    del pt_ref
    page_refs = refs[:CMP_PAGES]
    pos_ref, w1_ref, w2_ref, o_ref, lhs_scr, carry_scr = refs[CMP_PAGES:]
    grp = pl.program_id(1)
    nrow = CMP_CHUNKS
    width = B_GROUPS * HEAD_DIM
    half = CMP_STRIDE * HEAD_DIM

    @pl.when(grp == 0)
    def _():
        carry_scr[...] = jnp.zeros(carry_scr.shape, F32)

    row = lax.broadcasted_iota(jnp.int32, (nrow, CMP_HIDDEN), 0)
    for e in range(2):
        for g in range(B_GROUPS):
            for s in range(CMP_STRIDE):
                c0 = s * row_cols + (e * B_GROUPS + g) * HEAD_DIM
                piece = jnp.concatenate([pr[:, c0:c0 + HEAD_DIM] for pr in page_refs], axis=0)
                lhs_scr[g * nrow:(g + 1) * nrow, s * HEAD_DIM:(s + 1) * HEAD_DIM] = piece.astype(BF16)
        lhs = lhs_scr[...]
        first = _dot(lhs, w1_ref[e, 0:half, :])
        second = _dot(lhs, w1_ref[e, half:2 * half, :])
        bias = _dot(jnp.broadcast_to(pos_ref[e], (8, 2 * half)).astype(BF16), w1_ref[e])[0:1, :]
        for g in range(B_GROUPS):
            a = first[g * nrow:(g + 1) * nrow]
            cr = e * B_GROUPS + g
            prev = jnp.where(row == 0, carry_scr[cr:cr + 1, :], pltpu.roll(a, 1, 0))
            carry_scr[cr:cr + 1, :] = a[nrow - 1:nrow, :]
            pre = bias + prev + second[g * nrow:(g + 1) * nrow]
            o_ref[:, e * width + g * HEAD_DIM:e * width + (g + 1) * HEAD_DIM] = _dot(jax.nn.gelu(pre).astype(BF16), w2_ref[e])


def _compress(chunk_rows, page_ids, nseq, npages, row_cols, cmp_pos_flat, w1b, w2b):
    ngrp = npages // CMP_PAGES
    width = B_GROUPS * HEAD_DIM
    per_page = PAGE_SIZE // CMP_STRIDE

    def page_spec(k):
        return pl.BlockSpec((per_page, CMP_STRIDE * row_cols), lambda b, g, pt: (pt[b * npages + g * CMP_PAGES + k], 0))

    whole = lambda shape: pl.BlockSpec(shape, lambda b, g, pt: (0,) * len(shape))
    grid_spec = pltpu.PrefetchScalarGridSpec(
        num_scalar_prefetch=1,
        grid=(nseq, ngrp),
        in_specs=[page_spec(k) for k in range(CMP_PAGES)] + [
            whole((2, 1, CMP_LEN * HEAD_DIM)), whole((2, CMP_LEN * HEAD_DIM, CMP_HIDDEN)), whole((2, CMP_HIDDEN, HEAD_DIM))],
        out_specs=pl.BlockSpec((CMP_CHUNKS, 2 * width), lambda b, g, pt: (b * ngrp + g, 0)),
        scratch_shapes=[pltpu.VMEM((B_GROUPS * CMP_CHUNKS, CMP_STRIDE * HEAD_DIM), BF16),
                        pltpu.VMEM((8, CMP_HIDDEN), F32)],
    )
    return pl.pallas_call(
        functools.partial(_compress_kernel, row_cols=row_cols),
        grid_spec=grid_spec,
        out_shape=jax.ShapeDtypeStruct((nseq * ngrp * CMP_CHUNKS, 2 * width), F32),
        compiler_params=_params(("parallel", "arbitrary")),
        name="compress",
    )(page_ids, *([chunk_rows] * CMP_PAGES), cmp_pos_flat, w1b, w2b)


def _overlap_matrix(n_rows, n_sel):
    c = np.arange(n_rows)[:, None]
    j = np.arange(n_sel)[None, :]
    c_start = (c - 1) * CMP_STRIDE
    ov = (c >= 1) & (c_start < j * SLC_BLOCK + SLC_BLOCK) & (c_start + CMP_LEN > j * SLC_BLOCK)
    return ov.astype(np.float32)


NSA_TQ = 128
NSA_CK = 512


def _nsa_prompt_kernel(q_ref, kc_ref, vc_ref, ks_ref, vs_ref, kw_ref, vw_ref, gn_ref, ov_ref, ex_ref, o_ref,
                       ksb, vsb, kwb, vwb, m_scr, l_scr, acc_scr, *, seq):
    g = pl.program_id(1)
    t = pl.program_id(2)
    tq = NSA_TQ
    nb = seq // SLC_BLOCK

    @pl.when(t == 0)
    def _():
        ksb[...] = ks_ref[...].astype(BF16)
        vsb[...] = vs_ref[...].astype(BF16)
        kwb[...] = kw_ref[...].astype(BF16)
        vwb[...] = vw_ref[...].astype(BF16)

    rep = lambda x: jnp.concatenate([x] * GROUP_HEADS, axis=0)
    q4 = jnp.concatenate([q_ref[:, r * HEAD_DIM:(r + 1) * HEAD_DIM] for r in range(GROUP_HEADS)], axis=0).astype(BF16)
    qpos = t * tq + lax.broadcasted_iota(jnp.int32, (tq, 1), 0)
    qpos4 = rep(qpos)

    ncr = kc_ref.shape[0]
    sc = _dot_nt(q4, kc_ref[...].astype(BF16)) * SCALE
    cidx = lax.broadcasted_iota(jnp.int32, (1, ncr), 1)
    vis = (cidx >= 1) & (CMP_STRIDE * cidx + (CMP_LEN - 1 - CMP_STRIDE) <= qpos4)
    sc = jnp.where(vis, sc, NEG)
    pc = jnp.where(vis, jnp.exp(sc - jnp.max(sc, axis=1, keepdims=True)), 0.0)
    lc = jnp.sum(pc, axis=1, keepdims=True)
    pc = pc / jnp.where(lc > 0, lc, 1.0)
    o_c = _dot(pc.astype(BF16), vc_ref[...].astype(BF16))
    imp = pc[0:tq] + pc[tq:2 * tq] + pc[2 * tq:3 * tq] + pc[3 * tq:4 * tq]

    score = _dot_hi_lhs(imp, ov_ref[...])
    own = lax.shift_right_logical(qpos, 6)
    jdx = lax.broadcasted_iota(jnp.int32, (tq, nb), 1)
    forced = (jdx == 0) | (jdx == own - 1)
    score = jnp.where(jdx < own, jnp.where(forced, FORCE, score), NEG)
    sel = _rank_select(score, jdx, nb, SLC_TOPN - 1, axis=1)
    sel = jnp.where(jdx == own, 1.0, sel).astype(BF16)

    def chunk_scores(ci):
        st = pl.multiple_of(ci * NSA_CK, NSA_CK)
        s = _dot_nt(q4, ksb[pl.ds(st, NSA_CK), :]) * SCALE
        kpos = st + lax.broadcasted_iota(jnp.int32, (1, NSA_CK), 1)
        allowed = jnp.where(kpos <= qpos, _dot(sel, ex_ref[ci]), 0.0)
        return jnp.where(rep(allowed) > 0.5, s, NEG), st

    s, st = chunk_scores(0)
    m = jnp.max(s, axis=1, keepdims=True)
    p = jnp.exp(s - m)
    m_scr[...] = m
    l_scr[...] = jnp.sum(p, axis=1, keepdims=True)
    acc_scr[...] = _dot(p.astype(BF16), vsb[pl.ds(st, NSA_CK), :])

    def body(ci, carry):
        sj, stj = chunk_scores(ci)
        m_old = m_scr[...]
        m_new = jnp.maximum(m_old, jnp.max(sj, axis=1, keepdims=True))
        alpha = jnp.exp(m_old - m_new)
        pj = jnp.exp(sj - m_new)
        l_scr[...] = alpha * l_scr[...] + jnp.sum(pj, axis=1, keepdims=True)
        acc_scr[...] = alpha * acc_scr[...] + _dot(pj.astype(BF16), vsb[pl.ds(stj, NSA_CK), :])
        m_scr[...] = m_new
        return carry

    lax.fori_loop(1, (t * tq + tq - 1) // NSA_CK + 1, body, 0)
    o_s = acc_scr[...] / l_scr[...]

    span = WINDOW + tq
    ws = pl.multiple_of(jnp.maximum(t * tq - WINDOW, 0), tq)
    sw = _dot_nt(q4, kwb[pl.ds(ws, span), :]) * SCALE
    kpos = ws + lax.broadcasted_iota(jnp.int32, (1, span), 1)
    visw = (kpos <= qpos4) & (qpos4 - kpos < WINDOW)
    sw = jnp.where(visw, sw, NEG)
    pw = jnp.where(visw, jnp.exp(sw - jnp.max(sw, axis=1, keepdims=True)), 0.0)
    o_w = _dot(pw.astype(BF16), vwb[pl.ds(ws, span), :]) / jnp.sum(pw, axis=1, keepdims=True)

    sig = jax.nn.sigmoid(gn_ref[...])
    lane = lax.broadcasted_iota(jnp.int32, sig.shape, 1)
    for r in range(GROUP_HEADS):
        head = g * GROUP_HEADS + r
        gate = lambda br: jnp.sum(jnp.where(lane == br * B_HEADS + head, sig, 0.0), axis=1, keepdims=True)
        rows = slice(r * tq, (r + 1) * tq)
        y = gate(0) * o_c[rows] + gate(1) * o_s[rows] + gate(2) * o_w[rows]
        o_ref[:, r * HEAD_DIM:(r + 1) * HEAD_DIM] = y.astype(o_ref.dtype)


def _nsa_prompt(p_all, tok, gn, batch, seq):
    tq = NSA_TQ
    nt = seq // tq
    ncr = seq // CMP_STRIDE
    nb = seq // SLC_BLOCK
    nck = seq // NSA_CK
    ov = jnp.asarray(_overlap_matrix(ncr, nb), BF16)
    key_blk = (np.arange(seq) // SLC_BLOCK).reshape(nck, 1, NSA_CK)
    ex = jnp.asarray((key_blk == np.arange(nb)[None, :, None]).astype(np.float32), BF16)
    gw = B_GROUPS * HEAD_DIM
    col = lambda c0: (lambda b, g, t: (b, c0 + g))
    return pl.pallas_call(
        functools.partial(_nsa_prompt_kernel, seq=seq),
        grid=(batch, B_GROUPS, nt),
        in_specs=[pl.BlockSpec((tq, gw), lambda b, g, t: (b * nt + t, COL_QB // GROUP_HEADS + g)),
                  pl.BlockSpec((ncr, HEAD_DIM), col(0)),
                  pl.BlockSpec((ncr, HEAD_DIM), col(B_GROUPS)),
                  pl.BlockSpec((seq, HEAD_DIM), col(COL_KS)),
                  pl.BlockSpec((seq, HEAD_DIM), col(COL_VS)),
                  pl.BlockSpec((seq, HEAD_DIM), col(COL_KW)),
                  pl.BlockSpec((seq, HEAD_DIM), col(COL_VW)),
                  pl.BlockSpec((tq, LANES), lambda b, g, t: (b * nt + t, 0)),
                  pl.BlockSpec((ncr, nb), lambda b, g, t: (0, 0)),
                  pl.BlockSpec((nck, nb, NSA_CK), lambda b, g, t: (0, 0, 0))],
        out_specs=pl.BlockSpec((tq, gw), lambda b, g, t: (b * nt + t, g)),
        out_shape=jax.ShapeDtypeStruct((batch * seq, B_HEADS * HEAD_DIM), BF16),
        scratch_shapes=[pltpu.VMEM((seq, HEAD_DIM), BF16)] * 4 + [
            pltpu.VMEM((GROUP_HEADS * tq, 1), F32), pltpu.VMEM((GROUP_HEADS * tq, 1), F32),
            pltpu.VMEM((GROUP_HEADS * tq, HEAD_DIM), F32)],
        compiler_params=_params(("parallel", "parallel", "arbitrary")),
        name="nsa_prompt",
    )(p_all, tok, tok, p_all, p_all, p_all, p_all, gn, ov, ex)


SMP_PAGES = 4


def _lane_token(shape):
    return lax.shift_right_logical(lax.broadcasted_iota(jnp.int32, shape, 1), 4)


def _pad_rows(x, rows=8):
    return jnp.concatenate([x, jnp.zeros((rows - x.shape[0], x.shape[1]), x.dtype)], axis=0)


def _new_token_mask(shape, n_new):
    row = lax.broadcasted_iota(jnp.int32, shape, 0)
    return (row < n_new) & (row <= _lane_token(shape))


def _moba_sample_kernel(pt_ref, *refs, past, n_new):
    del pt_ref
    kp = refs[:SMP_PAGES]
    vp = refs[SMP_PAGES:2 * SMP_PAGES]
    qbd_ref, kn_ref, vn_ref, o_ref, s_scr, gate_scr, bmax_scr, sel_scr, m_scr, l_scr, acc_scr = refs[2 * SMP_PAGES:]
    ph = pl.program_id(1)
    st = pl.program_id(2)
    nst = pl.num_programs(2)
    nblk = past // MOBA_BLOCK
    ppb = MOBA_BLOCK // PAGE_SIZE
    qbd = qbd_ref[0]

    @pl.when(ph == 0)
    def _():
        for kb in range(SMP_PAGES // ppb):
            ss = []
            for k in range(kb * ppb, (kb + 1) * ppb):
                s = _dot(kp[k][...].astype(BF16), qbd)
                s_scr[pl.ds(pl.multiple_of((st * SMP_PAGES + k) * PAGE_SIZE, PAGE_SIZE), PAGE_SIZE), :] = s
                ss.append(s)
            blk = st * (SMP_PAGES // ppb) + kb
            tot = functools.reduce(jnp.add, [jnp.sum(s, axis=0, keepdims=True) for s in ss])
            gate_scr[pl.ds(blk, 1), :] = tot * (1.0 / MOBA_BLOCK)
            bmax_scr[pl.ds(blk, 1), :] = functools.reduce(jnp.maximum, [jnp.max(s, axis=0, keepdims=True) for s in ss])

    @pl.when((ph == 1) & (st == 0))
    def _():
        gate = gate_scr[...]
        jdx = lax.broadcasted_iota(jnp.int32, gate.shape, 0)
        qblk = lax.shift_right_logical(past + _lane_token(gate.shape), 8)
        gate = jnp.where(jdx < qblk, gate, NEG)
        sel = _rank_select(gate, jdx, nblk, MOBA_TOPK, axis=0)
        sel_scr[...] = sel
        s_own = _dot(_pad_rows(kn_ref[0]).astype(BF16), qbd) * SCALE
        own = _new_token_mask(s_own.shape, n_new)
        m = jnp.maximum(jnp.max(jnp.where(sel > 0.5, bmax_scr[...] * SCALE, NEG), axis=0, keepdims=True),
                        jnp.max(jnp.where(own, s_own, NEG), axis=0, keepdims=True))
        p_own = jnp.where(own, jnp.exp(s_own - m), 0.0)
        m_scr[...] = m
        l_scr[...] = jnp.sum(p_own, axis=0, keepdims=True)
        acc_scr[...] = _dot_tn(p_own.astype(BF16), _pad_rows(vn_ref[0]).astype(BF16))

    @pl.when(ph == 1)
    def _():
        m = m_scr[...]
        for k in range(SMP_PAGES):
            blk = st * (SMP_PAGES // ppb) + k // ppb
            s = s_scr[pl.ds(pl.multiple_of((st * SMP_PAGES + k) * PAGE_SIZE, PAGE_SIZE), PAGE_SIZE), :]
            p = jnp.where(sel_scr[pl.ds(blk, 1), :] > 0.5, jnp.exp(s * SCALE - m), 0.0)
            l_scr[...] += jnp.sum(p, axis=0, keepdims=True)
            acc_scr[...] += _dot_tn(p.astype(BF16), vp[k][...].astype(BF16))

    @pl.when((ph == 1) & (st == nst - 1))
    def _():
        l = l_scr[...]
        for t in range(n_new):
            for h in range(A_HEADS):
                i = t * A_HEADS + h
                cols = slice(h * HEAD_DIM, (h + 1) * HEAD_DIM)
                o_ref[0, t:t + 1, cols] = acc_scr[i:i + 1, cols] / l[:, i:i + 1]


def _moba_sample(cache_k2d, cache_v2d, page_ids, qbd, kn, vn, past):
    nseq, n_new, width = kn.shape
    npages = past // PAGE_SIZE
    nst = npages // SMP_PAGES
    nblk = past // MOBA_BLOCK

    def kspec(k):
        return pl.BlockSpec((PAGE_SIZE, width),
                            lambda b, ph, s, pt: (pt[b * npages + jnp.where(ph == 0, s, nst - 1) * SMP_PAGES + k], 0))

    def vspec(k):
        return pl.BlockSpec((PAGE_SIZE, width),
                            lambda b, ph, s, pt: (pt[b * npages + jnp.where(ph == 1, s, 0) * SMP_PAGES + k], 0))

    per_seq = lambda shape: pl.BlockSpec((1,) + shape, lambda b, ph, s, pt: (b, 0, 0))
    grid_spec = pltpu.PrefetchScalarGridSpec(
        num_scalar_prefetch=1,
        grid=(nseq, 2, nst),
        in_specs=[kspec(k) for k in range(SMP_PAGES)] + [vspec(k) for k in range(SMP_PAGES)] + [
            per_seq((width, LANES)), per_seq((n_new, width)), per_seq((n_new, width))],
        out_specs=per_seq((n_new, width)),
        scratch_shapes=[pltpu.VMEM((past, LANES), F32), pltpu.VMEM((nblk, LANES), F32), pltpu.VMEM((nblk, LANES), F32),
                        pltpu.VMEM((nblk, LANES), F32), pltpu.VMEM((1, LANES), F32), pltpu.VMEM((1, LANES), F32),
                        pltpu.VMEM((LANES, width), F32)],
    )
    return pl.pallas_call(
        functools.partial(_moba_sample_kernel, past=past, n_new=n_new),
        grid_spec=grid_spec,
        out_shape=jax.ShapeDtypeStruct((nseq, n_new, width), F32),
        compiler_params=_params(("parallel", "arbitrary", "arbitrary")),
        name="moba_sample",
    )(page_ids, *([cache_k2d] * SMP_PAGES), *([cache_v2d] * SMP_PAGES), qbd, kn, vn)


def _nsa_sample_kernel(pt_ref, *refs, past, n_new):
    del pt_ref
    pages = refs[:SMP_PAGES]
    (qbd_ref, tok_ref, new_ref, win_ref, gate_ref, ov_ref, gsum_ref, o_ref,
     s_scr, v_scr, sel_scr) = refs[SMP_PAGES:]
    st = pl.program_id(1)
    nst = pl.num_programs(1)
    gw = B_GROUPS * HEAD_DIM
    nb = past // SLC_BLOCK
    qbd = qbd_ref[0]

    for k in range(SMP_PAGES):
        rows = pl.ds(pl.multiple_of((st * SMP_PAGES + k) * PAGE_SIZE, PAGE_SIZE), PAGE_SIZE)
        s_scr[rows, :] = _dot(pages[k][:, 0:gw].astype(BF16), qbd)
        v_scr[rows, :] = pages[k][:, gw:2 * gw].astype(BF16)

    @pl.when(st == nst - 1)
    def _():
        shape = (8, LANES)
        tok_lane = _lane_token((1, LANES))
        qpos = past + tok_lane
        new = new_ref[0]
        pad_bf = lambda x: _pad_rows(x).astype(BF16)
        newmask = _new_token_mask(shape, n_new)

        ncr = tok_ref.shape[0]
        sc = _dot(tok_ref[:, 0:gw].astype(BF16), qbd) * SCALE
        cidx = lax.broadcasted_iota(jnp.int32, (ncr, 1), 0)
        vis = (cidx >= 1) & (CMP_STRIDE * cidx + (CMP_LEN - 1 - CMP_STRIDE) <= qpos)
        sc = jnp.where(vis, sc, NEG)
        pc = jnp.where(vis, jnp.exp(sc - jnp.max(sc, axis=0, keepdims=True)), 0.0)
        lc = jnp.sum(pc, axis=0, keepdims=True)
        pc = pc / jnp.where(lc > 0, lc, 1.0)
        o_c = _dot_tn(pc.astype(BF16), tok_ref[:, gw:2 * gw].astype(BF16))
        imp = _dot_hi_lhs(pc, gsum_ref[...])
        score = _dot_hi_rhs(ov_ref[...], imp)
        own = lax.shift_right_logical(qpos, 6)
        jdx = lax.broadcasted_iota(jnp.int32, score.shape, 0)
        forced = (jdx == 0) | (jdx == own - 1)
        score = jnp.where(jdx < own, jnp.where(forced, FORCE, score), NEG)
        sel_scr[...] = _rank_select(score, jdx, nb, SLC_TOPN - 1, axis=0)

        def block_max(j, m):
            blk = s_scr[pl.ds(pl.multiple_of(j * SLC_BLOCK, SLC_BLOCK), SLC_BLOCK), :]
            bm = jnp.max(blk, axis=0, keepdims=True) * SCALE
            return jnp.maximum(m, jnp.where(sel_scr[pl.ds(j, 1), :] > 0.5, bm, NEG))

        s_own = _dot(pad_bf(new[:, 0:gw]), qbd) * SCALE
        m = lax.fori_loop(0, nb, block_max, jnp.max(jnp.where(newmask, s_own, NEG), axis=0, keepdims=True))

        def block_prob(j, l):
            rows = pl.ds(pl.multiple_of(j * SLC_BLOCK, SLC_BLOCK), SLC_BLOCK)
            p = jnp.where(sel_scr[pl.ds(j, 1), :] > 0.5, jnp.exp(s_scr[rows, :] * SCALE - m), 0.0)
            s_scr[rows, :] = p
            return l + jnp.sum(p, axis=0, keepdims=True)

        p_own = jnp.where(newmask, jnp.exp(s_own - m), 0.0)
        l_s = lax.fori_loop(0, nb, block_prob, jnp.sum(p_own, axis=0, keepdims=True))
        o_s = _dot_tn(s_scr[...].astype(BF16), v_scr[...]) + _dot_tn(p_own.astype(BF16), pad_bf(new[:, gw:2 * gw]))

        wb = win_ref.shape[1]
        sw = _dot(win_ref[0, :, 0:gw].astype(BF16), qbd) * SCALE
        kpos = (past - wb) + lax.broadcasted_iota(jnp.int32, (wb, 1), 0)
        visw = (kpos <= qpos) & (qpos - kpos < WINDOW)
        swn = _dot(pad_bf(new[:, 2 * gw:3 * gw]), qbd) * SCALE
        mw = jnp.maximum(jnp.max(jnp.where(visw, sw, NEG), axis=0, keepdims=True),
                         jnp.max(jnp.where(newmask, swn, NEG), axis=0, keepdims=True))
        pw = jnp.where(visw, jnp.exp(sw - mw), 0.0)
        pwn = jnp.where(newmask, jnp.exp(swn - mw), 0.0)
        l_w = jnp.sum(pw, axis=0, keepdims=True) + jnp.sum(pwn, axis=0, keepdims=True)
        o_w = _dot_tn(pw.astype(BF16), win_ref[0, :, gw:2 * gw].astype(BF16)) + _dot_tn(pwn.astype(BF16), pad_bf(new[:, 3 * gw:4 * gw]))

        sig = jax.nn.sigmoid(gate_ref[0])
        for t in range(n_new):
            for h in range(B_HEADS):
                i = t * B_HEADS + h
                g = h // GROUP_HEADS
                cols = slice(g * HEAD_DIM, (g + 1) * HEAD_DIM)
                y = (sig[i:i + 1, 0:1] * o_c[i:i + 1, cols]
                     + sig[i:i + 1, 1:2] * (o_s[i:i + 1, cols] / l_s[:, i:i + 1])
                     + sig[i:i + 1, 2:3] * (o_w[i:i + 1, cols] / l_w[:, i:i + 1]))
                o_ref[0, t:t + 1, h * HEAD_DIM:(h + 1) * HEAD_DIM] = y


def _nsa_sample(cache2d, page_ids, qbd, tok, new, win, gates, past):
    nseq, n_new, _ = new.shape
    npages = past // PAGE_SIZE
    nst = npages // SMP_PAGES
    gw = B_GROUPS * HEAD_DIM
    nb = past // SLC_BLOCK
    ncr = tok.shape[0] // nseq
    wb = win.shape[1]
    ov = jnp.asarray(_overlap_matrix(ncr, nb).T, BF16)
    lane = np.arange(LANES)
    same = (lane[:, None] // B_HEADS == lane[None, :] // B_HEADS) & \
           ((lane[:, None] % B_HEADS) // GROUP_HEADS == (lane[None, :] % B_HEADS) // GROUP_HEADS)
    gsum = jnp.asarray(same.astype(np.float32), BF16)

    def pspec(k):
        return pl.BlockSpec((PAGE_SIZE, 2 * gw), lambda b, s, pt: (pt[b * npages + s * SMP_PAGES + k], 1))

    per_seq = lambda shape: pl.BlockSpec((1,) + shape, lambda b, s, pt: (b, 0, 0))
    grid_spec = pltpu.PrefetchScalarGridSpec(
        num_scalar_prefetch=1,
        grid=(nseq, nst),
        in_specs=[pspec(k) for k in range(SMP_PAGES)] + [
            per_seq((gw, LANES)),
            pl.BlockSpec((ncr, 2 * gw), lambda b, s, pt: (b, 0)),
            per_seq((n_new, 4 * gw)), per_seq((wb, 2 * gw)), per_seq((LANES, LANES)),
            pl.BlockSpec((nb, ncr), lambda b, s, pt: (0, 0)),
            pl.BlockSpec((LANES, LANES), lambda b, s, pt: (0, 0))],
        out_specs=per_seq((n_new, B_HEADS * HEAD_DIM)),
        scratch_shapes=[pltpu.VMEM((past, LANES), F32), pltpu.VMEM((past, gw), BF16), pltpu.VMEM((nb, LANES), F32)],
    )
    return pl.pallas_call(
        functools.partial(_nsa_sample_kernel, past=past, n_new=n_new),
        grid_spec=grid_spec,
        out_shape=jax.ShapeDtypeStruct((nseq, n_new, B_HEADS * HEAD_DIM), F32),
        compiler_params=_params(("parallel", "arbitrary")),
        name="nsa_sample",
    )(page_ids, *([cache2d] * SMP_PAGES), qbd, tok, new, win, gates, ov, gsum)


def _block_diag_queries(q, group_heads):
    nseq, n_tok, heads, d = q.shape
    groups = heads // group_heads
    onehot = jnp.asarray(np.arange(heads)[None, :] // group_heads == np.arange(groups)[:, None], q.dtype)
    qt = jnp.transpose(q, (0, 3, 1, 2))
    bd = onehot[None, :, None, None, :] * qt[:, None]
    bd = bd.reshape(nseq, groups * d, n_tok * heads)
    return jnp.pad(bd, ((0, 0), (0, 0), (0, LANES - n_tok * heads))).astype(BF16)


def _mix_kernel(ya_ref, yb_ref, ga_ref, gb_ref, wa_ref, wb_ref, wo_ref, x_ref, g_ref, b_ref, o_ref):
    c = pl.program_id(1)

    @pl.when(c == 0)
    def _():
        o_ref[...] = ALPHA * x_ref[...]

    u = jax.nn.sigmoid(ga_ref[...]) * _dot(ya_ref[...], wa_ref[...]) + jax.nn.sigmoid(gb_ref[...]) * _dot(yb_ref[...], wb_ref[...])
    o_ref[...] += _dot(u.astype(BF16), wo_ref[...])

    @pl.when(c == pl.num_programs(1) - 1)
    def _():
        o_ref[...] = _layer_norm(o_ref[...], g_ref[...], b_ref[...])


def _mix(ya, yb, gm, wa, wb, wo, x, ln_g, ln_b, tm, tc):
    m, d = x.shape
    wa_rows = wa.shape[0]
    nc = d // tc
    return pl.pallas_call(
        _mix_kernel,
        grid=(m // tm, nc),
        in_specs=[pl.BlockSpec((tm, wa_rows), lambda i, c: (i, 0)),
                  pl.BlockSpec((tm, wa_rows), lambda i, c: (i, 0)),
                  pl.BlockSpec((tm, tc), lambda i, c: (i, c)),
                  pl.BlockSpec((tm, tc), lambda i, c: (i, nc + c)),
                  pl.BlockSpec((wa_rows, tc), lambda i, c: (0, c)),
                  pl.BlockSpec((wa_rows, tc), lambda i, c: (0, c)),
                  pl.BlockSpec((tc, d), lambda i, c: (c, 0)),
                  pl.BlockSpec((tm, d), lambda i, c: (i, 0)),
                  pl.BlockSpec((1, d), lambda i, c: (0, 0)),
                  pl.BlockSpec((1, d), lambda i, c: (0, 0))],
        out_specs=pl.BlockSpec((tm, d), lambda i, c: (i, 0)),
        out_shape=jax.ShapeDtypeStruct((m, d), F32),
        compiler_params=_params(("parallel", "arbitrary")),
        name="merge_out_ln",
    )(ya, yb, gm, gm, wa, wb, wo, x, ln_g, ln_b)


def _ffn_kernel(h_ref, wg_ref, wu_ref, wd_ref, g_ref, b_ref, o_ref, hb):
    c = pl.program_id(1)

    @pl.when(c == 0)
    def _():
        hb[...] = h_ref[...].astype(BF16)
        o_ref[...] = ALPHA * h_ref[...]

    a = _dot(hb[...], wg_ref[...])
    act = a * jax.nn.sigmoid(a) * _dot(hb[...], wu_ref[...])
    o_ref[...] += _dot(act.astype(BF16), wd_ref[...])

    @pl.when(c == pl.num_programs(1) - 1)
    def _():
        o_ref[...] = _layer_norm(o_ref[...], g_ref[...], b_ref[...])


def _ffn(h, wg, wu, wd, ln_g, ln_b, tm, tc):
    m, d = h.shape
    dff = wg.shape[1]
    return pl.pallas_call(
        _ffn_kernel,
        grid=(m // tm, dff // tc),
        in_specs=[pl.BlockSpec((tm, d), lambda i, c: (i, 0)),
                  pl.BlockSpec((d, tc), lambda i, c: (0, c)),
                  pl.BlockSpec((d, tc), lambda i, c: (0, c)),
                  pl.BlockSpec((tc, d), lambda i, c: (c, 0)),
                  pl.BlockSpec((1, d), lambda i, c: (0, 0)),
                  pl.BlockSpec((1, d), lambda i, c: (0, 0))],
        out_specs=pl.BlockSpec((tm, d), lambda i, c: (i, 0)),
        out_shape=jax.ShapeDtypeStruct((m, d), F32),
        scratch_shapes=[pltpu.VMEM((tm, d), BF16)],
        compiler_params=_params(("parallel", "arbitrary")),
        name="swiglu_ln",
    )(h, wg, wu, wd, ln_g, ln_b)


def _rope_tables(pos):
    inv = ROPE_THETA ** (-jnp.arange(0, HEAD_DIM, 2, dtype=F32) / HEAD_DIM)
    ang = pos.astype(F32)[:, None] * inv[None, :]
    cos, sin = jnp.cos(ang), jnp.sin(ang)
    return jnp.concatenate([cos, cos], axis=-1), jnp.concatenate([-sin, sin], axis=-1)


def _row_tile(m, target):
    return max(t for t in range(16, target + 1, 16) if m % t == 0)


def kernel(x_prompt, x_sample, cache_moba_k, cache_moba_v, cache_nsa_kv, state_win_kv, page_table, w_in, cmp_pos, cmp_w1, cmp_w2, w_up_a, w_up_b, w_o, ln1_g, ln1_b, w_gate, w_up, w_down, ln2_g, ln2_b):
    batch, seq, d = x_prompt.shape
    nseq, n_new, _ = x_sample.shape
    past = page_table.shape[1] * PAGE_SIZE
    mp, ms = batch * seq, nseq * n_new
    gw = B_GROUPS * HEAD_DIM

    x_all = jnp.concatenate([x_prompt.reshape(mp, d), x_sample.reshape(ms, d)], axis=0)
    xb = x_all.astype(BF16)
    pos = jnp.concatenate([jnp.tile(jnp.arange(seq), batch), jnp.tile(past + jnp.arange(n_new), nseq)])
    cos, sin = _rope_tables(pos)

    w_main = w_in[:, :MAIN_COLS].astype(BF16)
    w_gn = jnp.pad(w_in[:, MAIN_COLS:MAIN_COLS + GATE_NSA_COLS], ((0, 0), (0, LANES - GATE_NSA_COLS))).astype(BF16)
    w_gm = w_in[:, MAIN_COLS + GATE_NSA_COLS:].astype(BF16)
    tm = _row_tile(mp + ms, 1024)
    tn = 512
    u = tn // HEAD_DIM
    rope_cols = [(COL_QA, COL_VA), (COL_QB, COL_VC), (COL_KS, COL_VS), (COL_KW, COL_VW)]
    rope_tiles = tuple((lo // u, hi // u) for lo, hi in rope_cols)
    p_all = _project(xb, w_main, cos, sin, rope_tiles, tm, tn)
    g_merge = _project(xb, w_gm, cos, sin, (), tm, tn)
    g_nsa = _project(xb, w_gn, cos, sin, (), tm, LANES)

    ya_p = _moba_prompt(p_all, batch, seq)
    w1b = cmp_w1.astype(BF16)
    w2b = cmp_w2.astype(BF16)
    pos_flat = cmp_pos.reshape(2, 1, CMP_LEN * HEAD_DIM)
    prompt_pages = jnp.arange(mp // PAGE_SIZE, dtype=jnp.int32)
    kvc_p = p_all[:mp, COL_KC * HEAD_DIM:COL_KS * HEAD_DIM].reshape(mp // CMP_STRIDE, CMP_STRIDE * 2 * gw)
    tok_p = _compress(kvc_p, prompt_pages, batch, seq // PAGE_SIZE, 2 * gw, pos_flat, w1b, w2b)
    yb_p = _nsa_prompt(p_all, tok_p, g_nsa, batch, seq)

    ps = p_all[mp:]
    head4 = lambda c0, n: ps[:, c0 * HEAD_DIM:(c0 + n) * HEAD_DIM].reshape(nseq, n_new, n, HEAD_DIM)
    page_ids = page_table.reshape(-1).astype(jnp.int32)
    n_pool = cache_moba_k.shape[0]
    qbd_a = _block_diag_queries(head4(COL_QA, A_HEADS), 1)
    ya_s = _moba_sample(cache_moba_k.reshape(n_pool * PAGE_SIZE, A_HEADS * HEAD_DIM),
                        cache_moba_v.reshape(n_pool * PAGE_SIZE, A_HEADS * HEAD_DIM), page_ids, qbd_a,
                        ps[:, COL_KA * HEAD_DIM:COL_VA * HEAD_DIM].reshape(nseq, n_new, -1),
                        ps[:, COL_VA * HEAD_DIM:COL_QB * HEAD_DIM].reshape(nseq, n_new, -1), past)
    cache2d = cache_nsa_kv.reshape(n_pool * PAGE_SIZE, 4 * gw)
    tok_s = _compress(cache_nsa_kv.reshape(n_pool * PAGE_SIZE // CMP_STRIDE, CMP_STRIDE * 4 * gw), page_ids, nseq,
                      past // PAGE_SIZE, 4 * gw, pos_flat, w1b, w2b)
    qbd_b = _block_diag_queries(head4(COL_QB, B_HEADS), GROUP_HEADS)
    gates_s = g_nsa[mp:, :GATE_NSA_COLS].reshape(nseq, n_new, 3, B_HEADS)
    gates_s = jnp.transpose(gates_s, (0, 1, 3, 2)).reshape(nseq, n_new * B_HEADS, 3)
    gates_s = jnp.pad(gates_s, ((0, 0), (0, LANES - n_new * B_HEADS), (0, LANES - 3)))
    yb_s = _nsa_sample(cache2d, page_ids, qbd_b, tok_s,
                       ps[:, COL_KS * HEAD_DIM:].reshape(nseq, n_new, 4 * gw),
                       state_win_kv.reshape(nseq, state_win_kv.shape[1], 2 * gw), gates_s, past)

    ya = jnp.concatenate([ya_p, ya_s.reshape(ms, -1).astype(BF16)], axis=0)
    yb = jnp.concatenate([yb_p, yb_s.reshape(ms, -1).astype(BF16)], axis=0)

    row = lambda v: v.reshape(1, d)
    h = _mix(ya, yb, g_merge, w_up_a.astype(BF16), w_up_b.astype(BF16), w_o.astype(BF16), x_all,
             row(ln1_g), row(ln1_b), _row_tile(mp + ms, 320), 512)
    y = _ffn(h, w_gate.astype(BF16), w_up.astype(BF16), w_down.astype(BF16), row(ln2_g), row(ln2_b),
             _row_tile(mp + ms, 416), 256)

    pp = p_all[:mp]
    cols = lambda a, lo, hi: a[:, lo * HEAD_DIM:hi * HEAD_DIM]
    wb = min(WINDOW, seq)
    win_new = cols(ps, COL_KW, COL_VW + B_GROUPS).reshape(nseq, n_new, 2, B_GROUPS, HEAD_DIM)
    return (y[:mp].reshape(batch, seq, d), y[mp:].reshape(nseq, n_new, d),
            cols(pp, COL_KA, COL_VA).reshape(batch, seq, A_HEADS, HEAD_DIM),
            cols(pp, COL_VA, COL_QB).reshape(batch, seq, A_HEADS, HEAD_DIM),
            cols(pp, COL_KC, COL_KW).reshape(batch, seq, 4, B_GROUPS, HEAD_DIM),
            cols(pp, COL_KW, COL_VW + B_GROUPS).reshape(batch, seq, 2, B_GROUPS, HEAD_DIM)[:, seq - wb:],
            cols(ps, COL_KA, COL_VA).reshape(nseq, n_new, A_HEADS, HEAD_DIM),
            cols(ps, COL_VA, COL_QB).reshape(nseq, n_new, A_HEADS, HEAD_DIM),
            cols(ps, COL_KC, COL_KW).reshape(nseq, n_new, 4, B_GROUPS, HEAD_DIM),
            jnp.concatenate([state_win_kv[:, n_new:], win_new], axis=1))
```
